```python
import jax, jax.numpy as jnp
from jax import lax
import numpy as np

D_MODEL = 1024
BATCH = 8
SEQ = 4096
DEPTH = 1
DEC_BATCH = 32
DEC_SEQ = 2048
PAST_LEN = 128

HEAD_DIM = 64
A_HEADS = D_MODEL // 128
A_KV_HEADS = 2
A_GROUP = A_HEADS // A_KV_HEADS
B_HEADS = D_MODEL // 128
D_FF = 4 * D_MODEL
GRID_W = 64
Q_BLOCK = 128
NA_WIN_ROWS = 8
NA_WIN_COLS = 16
ROPE_THETA = 10000.0
AXIS_ROPE_DIM = HEAD_DIM // 2
EPS = 1e-6
NEG_INF = -1e30

A_Q_W = A_HEADS * HEAD_DIM
A_KV_W = A_KV_HEADS * HEAD_DIM
B_W = B_HEADS * HEAD_DIM
W_IN_SPLITS = (A_Q_W, A_KV_W, A_KV_W, B_W, B_W, B_W, D_MODEL, D_MODEL)
W_IN_COLS = sum(W_IN_SPLITS)

kernel_name = "gated_gqa_axialrope_natten_encoder"


def _rmsnorm(x, g):
    xf = x.astype(jnp.float32)
    y = xf * lax.rsqrt(jnp.mean(xf * xf, axis=-1, keepdims=True) + EPS)
    return (y * g.astype(jnp.float32)).astype(x.dtype)


def _rope_tables(seq_len):
    t = jnp.arange(seq_len, dtype=jnp.int32)
    row = (t // GRID_W).astype(jnp.float32)
    col = (t % GRID_W).astype(jnp.float32)
    inv = ROPE_THETA ** (-jnp.arange(0, AXIS_ROPE_DIM, 2, dtype=jnp.float32) / AXIS_ROPE_DIM)
    ang = jnp.concatenate([row[:, None] * inv, col[:, None] * inv], axis=-1)
    return jnp.cos(ang), jnp.sin(ang)


def _apply_rope(x, cos, sin):
    shp = x.shape
    xf = x.astype(jnp.float32).reshape(shp[:-1] + (shp[-1] // 2, 2))
    bshape = (1, shp[1]) + (1,) * (x.ndim - 3) + (shp[-1] // 2,)
    c = cos.reshape(bshape)
    s = sin.reshape(bshape)
    x0, x1 = xf[..., 0], xf[..., 1]
    out = jnp.stack([x0 * c - x1 * s, x0 * s + x1 * c], axis=-1)
    return out.reshape(shp).astype(x.dtype)


def _gqa_attention(q, k, v):
    B, S = q.shape[0], q.shape[1]
    nblk = S // Q_BLOCK
    scale = HEAD_DIM ** -0.5
    qb = q.reshape(B, nblk, Q_BLOCK, A_KV_HEADS, A_GROUP, HEAD_DIM).swapaxes(0, 1)

    def block(qi):
        s = jnp.einsum('bqkgd,bskd->bkgqs', qi, k, preferred_element_type=jnp.float32) * scale
        p = jax.nn.softmax(s, axis=-1)
        return jnp.einsum('bkgqs,bskd->bqkgd', p.astype(v.dtype), v)

    o = lax.map(block, qb)
    return o.swapaxes(0, 1).reshape(B, S, A_Q_W)


def _neighbourhood_attention(q, k, v, rpb):
    B, S = q.shape[0], q.shape[1]
    rows = S // GRID_W
    wr = min(NA_WIN_ROWS, rows)
    q_rows = Q_BLOCK // GRID_W
    band = min(wr + q_rows - 1, rows)
    nkeys = band * GRID_W
    nblk = S // Q_BLOCK
    scale = HEAD_DIM ** -0.5
    kg = k.reshape(B, rows, GRID_W, B_HEADS, HEAD_DIM)
    vg = v.reshape(B, rows, GRID_W, B_HEADS, HEAD_DIM)
    qb = q.reshape(B, nblk, Q_BLOCK, B_HEADS, HEAD_DIM).swapaxes(0, 1)

    q_r_local = jnp.arange(Q_BLOCK, dtype=jnp.int32) // GRID_W
    q_c = jnp.arange(Q_BLOCK, dtype=jnp.int32) % GRID_W
    k_r_local = jnp.arange(nkeys, dtype=jnp.int32) // GRID_W
    k_c = jnp.arange(nkeys, dtype=jnp.int32) % GRID_W
    col_start = jnp.clip(q_c - NA_WIN_COLS // 2, 0, GRID_W - NA_WIN_COLS)
    col_mask = (k_c[None, :] >= col_start[:, None]) & (k_c[None, :] < col_start[:, None] + NA_WIN_COLS)
    dc_idx = jnp.clip(k_c[None, :] - q_c[:, None] + NA_WIN_COLS - 1, 0, 2 * NA_WIN_COLS - 2)

    def block(args):
        blk, qi = args
        q_r = blk * q_rows + q_r_local
        row_start = jnp.clip(q_r - wr // 2, 0, rows - wr)
        b0 = jnp.minimum(row_start[0], rows - band)
        kb = lax.dynamic_slice_in_dim(kg, b0, band, axis=1).reshape(B, nkeys, B_HEADS, HEAD_DIM)
        vb = lax.dynamic_slice_in_dim(vg, b0, band, axis=1).reshape(B, nkeys, B_HEADS, HEAD_DIM)
        k_r = b0 + k_r_local
        row_mask = (k_r[None, :] >= row_start[:, None]) & (k_r[None, :] < row_start[:, None] + wr)
        mask = row_mask & col_mask
        dr_idx = jnp.clip(k_r[None, :] - q_r[:, None] + NA_WIN_ROWS - 1, 0, 2 * NA_WIN_ROWS - 2)
        bias = rpb[:, dr_idx, dc_idx].astype(jnp.float32)
        s = jnp.einsum('bqhd,bkhd->bhqk', qi, kb, preferred_element_type=jnp.float32) * scale + bias[None]
        s = jnp.where(mask[None, None], s, NEG_INF)
        p = jax.nn.softmax(s, axis=-1)
        return jnp.einsum('bhqk,bkhd->bqhd', p.astype(vb.dtype), vb)

    o = lax.map(block, (jnp.arange(nblk, dtype=jnp.int32), qb))
    return o.swapaxes(0, 1).reshape(B, S, B_W)


def _token_mixer(xn, w_in, q_norm_g, k_norm_g, rpb, w_proj_a, w_proj_b, w_out):
    B, S, _ = xn.shape
    z = xn @ w_in
    idx = tuple(int(i) for i in np.cumsum(W_IN_SPLITS)[:-1])
    qa, ka, va, qb, kb, vb, ga, gb = jnp.split(z, idx, axis=-1)
    cos, sin = _rope_tables(S)
    qa = _apply_rope(_rmsnorm(qa.reshape(B, S, A_KV_HEADS, A_GROUP, HEAD_DIM), q_norm_g), cos, sin)
    ka = _apply_rope(_rmsnorm(ka.reshape(B, S, A_KV_HEADS, HEAD_DIM), k_norm_g), cos, sin)
    va = va.reshape(B, S, A_KV_HEADS, HEAD_DIM)
    o_a = _gqa_attention(qa, ka, va)
    o_b = _neighbourhood_attention(qb.reshape(B, S, B_HEADS, HEAD_DIM),
                                   kb.reshape(B, S, B_HEADS, HEAD_DIM),
                                   vb.reshape(B, S, B_HEADS, HEAD_DIM), rpb)
    merged = jax.nn.sigmoid(ga) * (o_a @ w_proj_a) + jax.nn.sigmoid(gb) * (o_b @ w_proj_b)
    return merged @ w_out


def _trunk(x, norm_mix_g, w_in, a_q_norm_g, a_k_norm_g, b_rel_pos_bias, w_proj_a, w_proj_b,
           w_out, norm_mlp_g, w_mlp_up, w_mlp_down, norm_final_g):
    h = x
    for l in range(DEPTH):
        h = h + _token_mixer(_rmsnorm(h, norm_mix_g[l]), w_in[l], a_q_norm_g[l], a_k_norm_g[l],
                             b_rel_pos_bias[l], w_proj_a[l], w_proj_b[l], w_out[l])
        hn = _rmsnorm(h, norm_mlp_g[l])
        h = h + jnp.square(jax.nn.relu(hn @ w_mlp_up[l])) @ w_mlp_down[l]
    return _rmsnorm(h, norm_final_g)


def setup_inputs(seed: int = 0) -> dict:
    key = jax.random.key(seed)
    ks = jax.random.split(key, 16)
    f32 = jnp.float32
    nrm = lambda k, shape, s: jax.random.normal(k, shape, f32) * s
    return {
        "x_prompt": nrm(ks[0], (BATCH, SEQ, D_MODEL), 1.0),
        "x_sample": nrm(ks[1], (DEC_BATCH, DEC_SEQ, D_MODEL), 1.0),
        "norm_mix_g": 1.0 + nrm(ks[2], (DEPTH, D_MODEL), 0.02),
        "w_in": nrm(ks[3], (DEPTH, D_MODEL, W_IN_COLS), D_MODEL ** -0.5),
        "a_q_norm_g": 1.0 + nrm(ks[4], (DEPTH, HEAD_DIM), 0.02),
        "a_k_norm_g": 1.0 + nrm(ks[5], (DEPTH, HEAD_DIM), 0.02),
        "b_rel_pos_bias": nrm(ks[6], (DEPTH, B_HEADS, 2 * NA_WIN_ROWS - 1, 2 * NA_WIN_COLS - 1), 0.1),
        "w_proj_a": nrm(ks[7], (DEPTH, A_Q_W, D_MODEL), A_Q_W ** -0.5),
        "w_proj_b": nrm(ks[8], (DEPTH, B_W, D_MODEL), B_W ** -0.5),
        "w_out": nrm(ks[9], (DEPTH, D_MODEL, D_MODEL), D_MODEL ** -0.5),
        "norm_mlp_g": 1.0 + nrm(ks[10], (DEPTH, D_MODEL), 0.02),
        "w_mlp_up": nrm(ks[11], (DEPTH, D_MODEL, D_FF), D_MODEL ** -0.5),
        "w_mlp_down": nrm(ks[12], (DEPTH, D_FF, D_MODEL), D_FF ** -0.5),
        "norm_final_g": 1.0 + nrm(ks[13], (D_MODEL,), 0.02),
    }


def reference(x_prompt, x_sample, norm_mix_g, w_in, a_q_norm_g, a_k_norm_g, b_rel_pos_bias,
              w_proj_a, w_proj_b, w_out, norm_mlp_g, w_mlp_up, w_mlp_down, norm_final_g):
    y_prompt = _trunk(x_prompt, norm_mix_g, w_in, a_q_norm_g, a_k_norm_g, b_rel_pos_bias,
                      w_proj_a, w_proj_b, w_out, norm_mlp_g, w_mlp_up, w_mlp_down, norm_final_g)
    y_sample = _trunk(x_sample, norm_mix_g, w_in, a_q_norm_g, a_k_norm_g, b_rel_pos_bias,
                      w_proj_a, w_proj_b, w_out, norm_mlp_g, w_mlp_up, w_mlp_down, norm_final_g)
    return (y_prompt, y_sample)
```

```python
import functools

import numpy as np
import jax
import jax.numpy as jnp
from jax import lax
from jax.experimental import pallas as pl
from jax.experimental.pallas import tpu as pltpu

F32 = jnp.float32
BF16 = jnp.bfloat16

D_MODEL = 1024
HEAD_DIM = 64
LANES = 128
A_HEADS = 8
A_KV_HEADS = 2
B_HEADS = 8
A_PAIRS = A_HEADS * HEAD_DIM // LANES
B_PAIRS = B_HEADS * HEAD_DIM // LANES
D_FF = 4 * D_MODEL
GRID_W = 64
NA_WIN_ROWS = 8
NA_WIN_COLS = 16
ROPE_THETA = 10000.0
EPS = 1e-6
NEG_INF = -1e30
SCALE = HEAD_DIM ** -0.5

A_Q_W = A_HEADS * HEAD_DIM
A_KV_W = A_KV_HEADS * HEAD_DIM
B_W = B_HEADS * HEAD_DIM

TOKEN_TILE = 512
GQA_Q_BLOCK = 256
GQA_K_CHUNK = 512
NA_Q_BLOCK = 256
NA_Q_ROWS = NA_Q_BLOCK // GRID_W
NA_BAND_ROWS = NA_WIN_ROWS + NA_Q_ROWS - 1
NA_BAND = NA_BAND_ROWS * GRID_W
VMEM_LIMIT = 56 * 1024 * 1024


def _const_spec(shape):
    n = len(shape)
    return pl.BlockSpec(shape, lambda *_: (0,) * n, pipeline_mode=pl.Buffered(1))


def _dot(a, b):
    return jnp.dot(a, b, preferred_element_type=F32)


def _dot_nt(a, b):
    return lax.dot_general(a, b, (((1,), (1,)), ((), ())), preferred_element_type=F32)


def _rmsnorm(x, g):
    return x * lax.rsqrt(jnp.mean(x * x, axis=-1, keepdims=True) + EPS) * g


def _head_rmsnorm(v, ones_bd, g):
    y = v * v
    hi = y.astype(BF16)
    lo = (y - hi.astype(F32)).astype(BF16)
    ss = _dot(hi, ones_bd) + _dot(lo, ones_bd)
    return v * lax.rsqrt(ss * (1.0 / HEAD_DIM) + EPS) * g


def _rope(v, c, sa, sb):
    w = v.shape[-1]
    return v * c + pltpu.roll(v, w - 1, 1) * sa + pltpu.roll(v, 1, 1) * sb


def _proj_in_kernel(x_ref, gmix_ref, wa_ref, wb_ref, wg_ref, gq_ref, gk_ref,
                    cos_ref, sa_ref, sb_ref, onesq_ref, onesk_ref,
                    qa_ref, kam_ref, vam_ref, qb_ref, kbm_ref, vbm_ref, sga_ref, sgb_ref):
    xb = _rmsnorm(x_ref[...], gmix_ref[...]).astype(BF16)
    c, sa, sb = cos_ref[...], sa_ref[...], sb_ref[...]
    lower = lax.broadcasted_iota(jnp.int32, (1, LANES), 1) < HEAD_DIM

    za = _dot(xb, wa_ref[...])
    q = _head_rmsnorm(za[:, :A_Q_W], onesq_ref[...], gq_ref[...])
    q = _rope(q, jnp.tile(c, (1, A_PAIRS)), jnp.tile(sa, (1, A_PAIRS)),
              jnp.tile(sb, (1, A_PAIRS))) * SCALE
    for p in range(A_PAIRS):
        qa_ref[0, p] = q[:, p * LANES:(p + 1) * LANES].astype(BF16)

    k = _head_rmsnorm(za[:, A_Q_W:A_Q_W + A_KV_W], onesk_ref[...], gk_ref[...])
    k = _rope(k, c, sa, sb)
    v = za[:, A_Q_W + A_KV_W:]
    for src, dst in ((k, kam_ref), (v, vam_ref)):
        swapped = pltpu.roll(src, HEAD_DIM, 1)
        dst[0, 0, 0] = jnp.where(lower, src, 0.0).astype(BF16)
        dst[0, 0, 1] = jnp.where(lower, 0.0, swapped).astype(BF16)
        dst[0, 1, 0] = jnp.where(lower, swapped, 0.0).astype(BF16)
        dst[0, 1, 1] = jnp.where(lower, 0.0, src).astype(BF16)

    zb = _dot(xb, wb_ref[...])
    for p in range(B_PAIRS):
        sl = slice(p * LANES, (p + 1) * LANES)
        qb_ref[0, p] = (zb[:, sl] * SCALE).astype(BF16)
        for off, dst in ((B_W, kbm_ref), (2 * B_W, vbm_ref)):
            t = zb[:, off + p * LANES: off + (p + 1) * LANES]
            dst[0, p, 0] = jnp.where(lower, t, 0.0).astype(BF16)
            dst[0, p, 1] = jnp.where(lower, 0.0, t).astype(BF16)

    zg = _dot(xb, wg_ref[...])
    sga_ref[...] = jax.nn.sigmoid(zg[:, :D_MODEL]).astype(BF16)
    sgb_ref[...] = jax.nn.sigmoid(zg[:, D_MODEL:]).astype(BF16)


def _proj_in(x2, prm, rope, batch, seq):
    tm = TOKEN_TILE
    nt = seq // tm
    tok = lambda i, b: (b * nt + i, 0)
    pos = lambda i, b: (i, 0)
    pair5 = lambda i, b: (b, 0, 0, i, 0)
    out_shape = (
        jax.ShapeDtypeStruct((batch, A_PAIRS, seq, LANES), BF16),
        jax.ShapeDtypeStruct((batch, A_KV_HEADS, 2, seq, LANES), BF16),
        jax.ShapeDtypeStruct((batch, A_KV_HEADS, 2, seq, LANES), BF16),
        jax.ShapeDtypeStruct((batch, B_PAIRS, seq, LANES), BF16),
        jax.ShapeDtypeStruct((batch, B_PAIRS, 2, seq, LANES), BF16),
        jax.ShapeDtypeStruct((batch, B_PAIRS, 2, seq, LANES), BF16),
        jax.ShapeDtypeStruct((batch * seq, D_MODEL), BF16),
        jax.ShapeDtypeStruct((batch * seq, D_MODEL), BF16),
    )
    out_specs = (
        pl.BlockSpec((1, A_PAIRS, tm, LANES), lambda i, b: (b, 0, i, 0)),
        pl.BlockSpec((1, A_KV_HEADS, 2, tm, LANES), pair5),
        pl.BlockSpec((1, A_KV_HEADS, 2, tm, LANES), pair5),
        pl.BlockSpec((1, B_PAIRS, tm, LANES), lambda i, b: (b, 0, i, 0)),
        pl.BlockSpec((1, B_PAIRS, 2, tm, LANES), pair5),
        pl.BlockSpec((1, B_PAIRS, 2, tm, LANES), pair5),
        pl.BlockSpec((tm, D_MODEL), tok),
        pl.BlockSpec((tm, D_MODEL), tok),
    )
    in_specs = [
        pl.BlockSpec((tm, D_MODEL), tok),
        _const_spec((1, D_MODEL)),
        _const_spec(prm["wa"].shape), _const_spec(prm["wb"].shape), _const_spec(prm["wg"].shape),
        _const_spec((1, A_Q_W)), _const_spec((1, LANES)),
        pl.BlockSpec((tm, LANES), pos), pl.BlockSpec((tm, LANES), pos), pl.BlockSpec((tm, LANES), pos),
        _const_spec((A_Q_W, A_Q_W)), _const_spec((LANES, LANES)),
    ]
    return pl.pallas_call(
        _proj_in_kernel, grid=(nt, batch), in_specs=in_specs, out_specs=out_specs,
        out_shape=out_shape, name="proj_in",
        compiler_params=pltpu.CompilerParams(
            dimension_semantics=("arbitrary", "arbitrary"), vmem_limit_bytes=VMEM_LIMIT),
    )(x2, prm["gmix"], prm["wa"], prm["wb"], prm["wg"], prm["gq"], prm["gk"],
      rope[0], rope[1], rope[2], prm["ones_q"], prm["ones_k"])


def _gqa_kernel(q_ref, k_ref, v_ref, o_ref, *, seq):
    bq, ck = GQA_Q_BLOCK, GQA_K_CHUNK
    rows = 2 * bq
    q = q_ref[0].reshape(rows, LANES)
    lower = lax.broadcasted_iota(jnp.int32, (1, LANES), 1) < HEAD_DIM

    def body(c, carry):
        m_a, l_a, m_b, l_b, acc = carry
        off = pl.multiple_of(c * ck, ck)
        s_a = _dot_nt(q, k_ref[0, 0, 0, pl.ds(off, ck), :])
        s_b = _dot_nt(q, k_ref[0, 0, 1, pl.ds(off, ck), :])
        mn_a = jnp.maximum(m_a, jnp.max(s_a, axis=-1, keepdims=True))
        mn_b = jnp.maximum(m_b, jnp.max(s_b, axis=-1, keepdims=True))
        p_a = jnp.exp(s_a - mn_a)
        p_b = jnp.exp(s_b - mn_b)
        al_a = jnp.exp(m_a - mn_a)
        al_b = jnp.exp(m_b - mn_b)
        l_a = al_a * l_a + jnp.sum(p_a, axis=-1, keepdims=True)
        l_b = al_b * l_b + jnp.sum(p_b, axis=-1, keepdims=True)
        pv = (_dot(p_a.astype(BF16), v_ref[0, 0, 0, pl.ds(off, ck), :])
              + _dot(p_b.astype(BF16), v_ref[0, 0, 1, pl.ds(off, ck), :]))
        acc = acc * jnp.where(lower, al_a, al_b) + pv
        return mn_a, l_a, mn_b, l_b, acc

    col = lambda val: jnp.full((rows, 1), val, F32)
    init = (col(NEG_INF), col(0.0), col(NEG_INF), col(0.0), jnp.zeros((rows, LANES), F32))
    _, l_a, _, l_b, acc = lax.fori_loop(0, seq // ck, body, init)
    out = acc * jnp.where(lower, 1.0 / l_a, 1.0 / l_b)
    o_ref[0] = out.astype(BF16).reshape(2, bq, LANES)


def _gqa(qa, kam, vam, batch, seq):
    bq = GQA_Q_BLOCK
    kv_spec = pl.BlockSpec((1, 1, 2, seq, LANES), lambda b, j, i: (b, j, 0, 0, 0))
    q_spec = pl.BlockSpec((1, 2, bq, LANES), lambda b, j, i: (b, j, i, 0))
    return pl.pallas_call(
        functools.partial(_gqa_kernel, seq=seq),
        grid=(batch, A_KV_HEADS, seq // bq),
        in_specs=[q_spec, kv_spec, kv_spec], out_specs=q_spec,
        out_shape=jax.ShapeDtypeStruct((batch, A_PAIRS, seq, LANES), BF16), name="gqa",
        compiler_params=pltpu.CompilerParams(
            dimension_semantics=("arbitrary",) * 3, vmem_limit_bytes=VMEM_LIMIT),
    )(qa, kam, vam)


def _na_band_start(blk, rows):
    return jnp.clip(NA_Q_ROWS * blk - NA_WIN_ROWS // 2, 0, rows - NA_BAND_ROWS)


def _natten_kernel(q_ref, k_ref, v_ref, tab_ref, o_ref, *, seq):
    rows = seq // GRID_W
    nblk = seq // NA_Q_BLOCK
    lower = lax.broadcasted_iota(jnp.int32, (1, LANES), 1) < HEAD_DIM

    def body(blk, carry):
        kind = jnp.where(blk == 0, 0, jnp.where(blk == nblk - 1, 2, 1))
        qoff = pl.multiple_of(blk * NA_Q_BLOCK, NA_Q_BLOCK)
        koff = pl.multiple_of(_na_band_start(blk, rows) * GRID_W, GRID_W)
        q = q_ref[0, 0, pl.ds(qoff, NA_Q_BLOCK), :]
        s_a = _dot_nt(q, k_ref[0, 0, 0, pl.ds(koff, NA_BAND), :]) + tab_ref[0, kind, 0]
        s_b = _dot_nt(q, k_ref[0, 0, 1, pl.ds(koff, NA_BAND), :]) + tab_ref[0, kind, 1]
        p_a = jnp.exp(s_a - jnp.max(s_a, axis=-1, keepdims=True))
        p_b = jnp.exp(s_b - jnp.max(s_b, axis=-1, keepdims=True))
        l_a = jnp.sum(p_a, axis=-1, keepdims=True)
        l_b = jnp.sum(p_b, axis=-1, keepdims=True)
        pv = (_dot(p_a.astype(BF16), v_ref[0, 0, 0, pl.ds(koff, NA_BAND), :])
              + _dot(p_b.astype(BF16), v_ref[0, 0, 1, pl.ds(koff, NA_BAND), :]))
        out = pv * jnp.where(lower, 1.0 / l_a, 1.0 / l_b)
        o_ref[0, 0, pl.ds(qoff, NA_Q_BLOCK), :] = out.astype(BF16)
        return carry

    lax.fori_loop(0, nblk, body, 0)


def _natten(qb, kbm, vbm, tab, batch, seq):
    q_spec = pl.BlockSpec((1, 1, seq, LANES), lambda b, p: (b, p, 0, 0))
    kv_spec = pl.BlockSpec((1, 1, 2, seq, LANES), lambda b, p: (b, p, 0, 0, 0))
    tab_spec = pl.BlockSpec((1, 3, 2, NA_Q_BLOCK, NA_BAND), lambda b, p: (p, 0, 0, 0, 0))
    return pl.pallas_call(
        functools.partial(_natten_kernel, seq=seq),
        grid=(batch, B_PAIRS),
        in_specs=[q_spec, kv_spec, kv_spec, tab_spec], out_specs=q_spec,
        out_shape=jax.ShapeDtypeStruct((batch, B_PAIRS, seq, LANES), BF16), name="natten",
        compiler_params=pltpu.CompilerParams(
            dimension_semantics=("arbitrary",) * 2, vmem_limit_bytes=VMEM_LIMIT),
    )(qb, kbm, vbm, tab)


def _na_tables(rpb, seq):
    rows = seq // GRID_W
    nblk = seq // NA_Q_BLOCK
    wr = min(NA_WIN_ROWS, rows)
    assert wr == NA_WIN_ROWS and nblk >= 3 and rows >= NA_BAND_ROWS
    qi = np.arange(NA_Q_BLOCK)
    ki = np.arange(NA_BAND)
    q_c, k_c = qi % GRID_W, ki % GRID_W
    col_start = np.clip(q_c - NA_WIN_COLS // 2, 0, GRID_W - NA_WIN_COLS)
    col_mask = (k_c[None, :] >= col_start[:, None]) & (k_c[None, :] < col_start[:, None] + NA_WIN_COLS)
    dc_idx = np.clip(k_c[None, :] - q_c[:, None] + NA_WIN_COLS - 1, 0, 2 * NA_WIN_COLS - 2)
    tabs = []
    for blk in (0, 1, nblk - 1):
        q_r = blk * NA_Q_ROWS + qi // GRID_W
        row_start = np.clip(q_r - wr // 2, 0, rows - wr)
        b0 = int(np.clip(NA_Q_ROWS * blk - NA_WIN_ROWS // 2, 0, rows - NA_BAND_ROWS))
        k_r = b0 + ki // GRID_W
        row_mask = (k_r[None, :] >= row_start[:, None]) & (k_r[None, :] < row_start[:, None] + wr)
        assert row_mask.sum(axis=1).min() == wr * GRID_W
        dr_idx = np.clip(k_r[None, :] - q_r[:, None] + NA_WIN_ROWS - 1, 0, 2 * NA_WIN_ROWS - 2)
        bias = rpb[:, dr_idx, dc_idx].astype(F32)
        tabs.append(jnp.where((row_mask & col_mask)[None], bias, NEG_INF))
    tab = jnp.stack(tabs, axis=0).reshape(3, B_PAIRS, 2, NA_Q_BLOCK, NA_BAND)
    return jnp.transpose(tab, (1, 0, 2, 3, 4))


def _post_kernel(x_ref, oa_ref, ob_ref, sga_ref, sgb_ref, wpa_ref, wpb_ref, wout_ref,
                 gmlp_ref, wup_ref, wdn_ref, gfin_ref, y_ref):
    oa = jnp.concatenate([oa_ref[0, p] for p in range(A_PAIRS)], axis=-1)
    ob = jnp.concatenate([ob_ref[0, p] for p in range(B_PAIRS)], axis=-1)
    merged = (sga_ref[...].astype(F32) * _dot(oa, wpa_ref[...])
              + sgb_ref[...].astype(F32) * _dot(ob, wpb_ref[...]))
    h = x_ref[...] + _dot(merged.astype(BF16), wout_ref[...])
    hn = _rmsnorm(h, gmlp_ref[...]).astype(BF16)
    acc = h
    for c in range(D_FF // D_MODEL):
        sl = slice(c * D_MODEL, (c + 1) * D_MODEL)
        u = jnp.square(jnp.maximum(_dot(hn, wup_ref[:, sl]), 0.0)).astype(BF16)
        acc = acc + _dot(u, wdn_ref[sl, :])
    y_ref[...] = _rmsnorm(acc, gfin_ref[...])


def _post(x2, oa, ob, sga, sgb, prm, batch, seq):
    tm = TOKEN_TILE
    nt = seq // tm
    tok = pl.BlockSpec((tm, D_MODEL), lambda b, i: (b * nt + i, 0))
    pair = pl.BlockSpec((1, A_PAIRS, tm, LANES), lambda b, i: (b, 0, i, 0))
    in_specs = [tok, pair, pair, tok, tok,
                _const_spec(prm["wpa"].shape), _const_spec(prm["wpb"].shape),
                _const_spec(prm["wout"].shape), _const_spec((1, D_MODEL)),
                _const_spec(prm["wup"].shape), _const_spec(prm["wdn"].shape),
                _const_spec((1, D_MODEL))]
    return pl.pallas_call(
        _post_kernel, grid=(batch, nt), in_specs=in_specs, out_specs=tok,
        out_shape=jax.ShapeDtypeStruct((batch * seq, D_MODEL), F32), name="post",
        compiler_params=pltpu.CompilerParams(
            dimension_semantics=("arbitrary", "arbitrary"), vmem_limit_bytes=VMEM_LIMIT),
    )(x2, oa, ob, sga, sgb, prm["wpa"], prm["wpb"], prm["wout"], prm["gmlp"],
      prm["wup"], prm["wdn"], prm["gfin"])


def _rope_tables(seq):
    t = jnp.arange(seq, dtype=jnp.int32)
    row = (t // GRID_W).astype(F32)
    col = (t % GRID_W).astype(F32)
    half = HEAD_DIM // 2
    inv = ROPE_THETA ** (-jnp.arange(0, half, 2, dtype=F32) / half)
    ang = jnp.concatenate([row[:, None] * inv, col[:, None] * inv], axis=-1)
    cos, sin = jnp.cos(ang), jnp.sin(ang)
    lane = np.arange(LANES)
    pair_idx = (lane % HEAD_DIM) // 2
    even = (lane % 2 == 0)[None, :]
    c = cos[:, pair_idx]
    s = sin[:, pair_idx]
    return c, jnp.where(even, -s, 0.0), jnp.where(even, 0.0, s)


def _block_ones(width):
    return jnp.asarray(np.kron(np.eye(width // HEAD_DIM), np.ones((HEAD_DIM, HEAD_DIM))), BF16)


def _trunk(x, prm, rpb):
    batch, seq, _ = x.shape
    x2 = x.reshape(batch * seq, D_MODEL)
    qa, kam, vam, qb, kbm, vbm, sga, sgb = _proj_in(x2, prm, _rope_tables(seq), batch, seq)
    oa = _gqa(qa, kam, vam, batch, seq)
    ob = _natten(qb, kbm, vbm, _na_tables(rpb, seq), batch, seq)
    y = _post(x2, oa, ob, sga, sgb, prm, batch, seq)
    return y.reshape(batch, seq, D_MODEL)


def kernel(x_prompt, x_sample, norm_mix_g, w_in, a_q_norm_g, a_k_norm_g, b_rel_pos_bias,
           w_proj_a, w_proj_b, w_out, norm_mlp_g, w_mlp_up, w_mlp_down, norm_final_g):
    assert w_in.shape[0] == 1, "single-layer trunk"
    w = w_in[0].astype(BF16)
    n_a = A_Q_W + 2 * A_KV_W
    row = lambda g: g.reshape(1, -1).astype(F32)
    prm = dict(
        gmix=row(norm_mix_g[0]),
        wa=w[:, :n_a], wb=w[:, n_a:n_a + 3 * B_W], wg=w[:, n_a + 3 * B_W:],
        gq=row(jnp.tile(a_q_norm_g[0], A_HEADS)), gk=row(jnp.tile(a_k_norm_g[0], A_KV_HEADS)),
        ones_q=_block_ones(A_Q_W), ones_k=_block_ones(LANES),
        wpa=w_proj_a[0].astype(BF16), wpb=w_proj_b[0].astype(BF16), wout=w_out[0].astype(BF16),
        gmlp=row(norm_mlp_g[0]), wup=w_mlp_up[0].astype(BF16), wdn=w_mlp_down[0].astype(BF16),
        gfin=row(norm_final_g),
    )
    rpb = b_rel_pos_bias[0]
    return (_trunk(x_prompt, prm, rpb), _trunk(x_sample, prm, rpb))
```
